```python
import math
import jax, jax.numpy as jnp
from jax import lax
import numpy as np

D_MODEL = 2048
BATCH = 2
SEQ = 4096
DEPTH = 4

CHUNK = 64
N_BRANCH = 3
BRANCH_W = 3 * D_MODEL // 8
D_IN = 9 * BRANCH_W
POOL_WINDOWS = (2, 4, 8, 16)
POOL_GROUPS = len(POOL_WINDOWS)
POOL_GW = BRANCH_W // POOL_GROUPS
DA_QK_DIM = 64
DA_V_DIM = 2 * DA_QK_DIM
DA_HEADS = BRANCH_W // DA_V_DIM
Q_BLOCK = 128
NUM_BUCKETS = 32
MAX_DISTANCE = 128
SGU_CHUNK = 128
SGU_GW = 128
SGU_GROUPS = BRANCH_W // SGU_GW
EPS = 1e-6
NEG_INF = -1e30

kernel_name = "hybrid_pool_diffattn_sgu_trunk"


def rmsnorm(x, g):
    xf = x.astype(jnp.float32)
    y = xf * lax.rsqrt(jnp.mean(xf * xf, axis=-1, keepdims=True) + EPS)
    return (y * g.astype(jnp.float32)).astype(x.dtype)


def layernorm(x, g, b):
    xf = x.astype(jnp.float32)
    mu = jnp.mean(xf, axis=-1, keepdims=True)
    var = jnp.mean(jnp.square(xf - mu), axis=-1, keepdims=True)
    y = (xf - mu) * lax.rsqrt(var + EPS)
    return (y * g.astype(jnp.float32) + b.astype(jnp.float32)).astype(x.dtype)


def pool_mixer(xa, w, scale):
    B, S, _ = xa.shape
    xg = xa.reshape(B, S, POOL_GROUPS, POOL_GW).astype(jnp.float32)
    c = jnp.cumsum(xg, axis=1)
    c = jnp.concatenate([jnp.zeros_like(c[:, :1]), c], axis=1)
    t = jnp.arange(S)
    outs = []
    for g, wl in enumerate(POOL_WINDOWS):
        cg = c[:, :, g]
        upper = cg[:, 1:]
        lower = jnp.concatenate([jnp.zeros_like(cg[:, :wl - 1]), cg[:, :S + 1 - wl]], axis=1)
        count = jnp.minimum(t + 1, wl).astype(jnp.float32)[None, :, None]
        outs.append((upper - lower) / count - xg[:, :, g])
    pooled = jnp.stack(outs, axis=2).astype(xa.dtype)
    mixed = jnp.einsum('bsgc,gcd->bsgd', pooled, w)
    return mixed.reshape(B, S, BRANCH_W) * scale


def t5_bucket(rel):
    half = NUM_BUCKETS // 2
    max_exact = half // 2
    n = jnp.abs(rel)
    nf = jnp.maximum(n, max_exact).astype(jnp.float32)
    large = max_exact + (jnp.log(nf / max_exact) / math.log(MAX_DISTANCE / max_exact)
                         * (half - max_exact)).astype(jnp.int32)
    large = jnp.minimum(large, half - 1)
    return jnp.where(rel > 0, half, 0) + jnp.where(n < max_exact, n, large)


def diff_attention(q, k, v, lam, subln_g, rel_bias, lambda_init):
    B, S, _ = q.shape
    q = q.reshape(B, S, DA_HEADS, 2, DA_QK_DIM)
    k = k.reshape(B, S, DA_HEADS, 2, DA_QK_DIM)
    v = v.reshape(B, S, DA_HEADS, DA_V_DIM)
    lamf = lam.astype(jnp.float32)
    lam_full = (jnp.exp(jnp.sum(lamf[0] * lamf[1])) - jnp.exp(jnp.sum(lamf[2] * lamf[3]))
                + lambda_init)
    n_blocks = S // Q_BLOCK
    qb = q.reshape(B, n_blocks, Q_BLOCK, DA_HEADS, 2, DA_QK_DIM).transpose(1, 0, 2, 3, 4, 5)
    k1, k2 = k[..., 0, :], k[..., 1, :]
    k_pos = jnp.arange(S)
    k_chunk = k_pos // CHUNK
    table = rel_bias.astype(jnp.float32)
    scale = DA_QK_DIM ** -0.5

    def block(args):
        qblk, bi = args
        q_pos = bi * Q_BLOCK + jnp.arange(Q_BLOCK)
        mask = k_chunk[None, :] <= (q_pos // CHUNK)[:, None]
        bias = table[t5_bucket(k_pos[None, :] - q_pos[:, None])]
        bias = bias.transpose(2, 0, 1)[None]
        s1 = jnp.einsum('bqhd,bkhd->bhqk', qblk[..., 0, :], k1).astype(jnp.float32) * scale + bias
        s2 = jnp.einsum('bqhd,bkhd->bhqk', qblk[..., 1, :], k2).astype(jnp.float32) * scale + bias
        p1 = jax.nn.softmax(jnp.where(mask, s1, NEG_INF), axis=-1)
        p2 = jax.nn.softmax(jnp.where(mask, s2, NEG_INF), axis=-1)
        attn = (p1 - lam_full * p2).astype(v.dtype)
        return jnp.einsum('bhqk,bkhd->bqhd', attn, v)

    o = lax.map(block, (qb, jnp.arange(n_blocks)))
    o = o.transpose(1, 0, 2, 3, 4).reshape(B, S, DA_HEADS, DA_V_DIM)
    o = rmsnorm(o, subln_g) * (1.0 - lambda_init)
    return o.reshape(B, S, BRANCH_W)


def spatial_gating(u, v, norm_g, norm_b, w_s, b_s):
    B, S, _ = u.shape
    v = layernorm(v, norm_g, norm_b)
    nc = S // SGU_CHUNK
    vc = v.reshape(B, nc, SGU_CHUNK, SGU_GROUPS, SGU_GW)
    w_m = w_s * jnp.tril(jnp.ones((SGU_CHUNK, SGU_CHUNK), w_s.dtype))
    mixed = jnp.einsum('gpq,bnqgc->bnpgc', w_m, vc) + b_s.T[None, None, :, :, None]
    return u * mixed.reshape(B, S, BRANCH_W)


def hybrid_layer(x, rel_bias, g_pre, w_in, w_gate, pool_w, pool_scale, lam, subln_g,
                 sgu_norm_g, sgu_norm_b, sgu_w, sgu_b, w_branch, w_out, g_post, lambda_init):
    B, S, D = x.shape
    h = rmsnorm(x, g_pre)
    proj = h @ w_in
    a_x, a_g, q, k, v, b_g, c_u, c_v, c_g = jnp.split(proj, 9, axis=-1)
    y_a = pool_mixer(a_x, pool_w, pool_scale) * jax.nn.silu(a_g)
    y_b = diff_attention(q, k, v, lam, subln_g, rel_bias, lambda_init) * jax.nn.silu(b_g)
    y_c = spatial_gating(jax.nn.gelu(c_u), jax.nn.gelu(c_v), sgu_norm_g, sgu_norm_b,
                         sgu_w, sgu_b) * jax.nn.silu(c_g)
    branches = jnp.stack([y_a, y_b, y_c], axis=2)
    up = jnp.einsum('bsnw,nwd->bsnd', branches, w_branch)
    gates = jax.nn.sigmoid(h @ w_gate).reshape(B, S, N_BRANCH, D)
    merged = jnp.sum(gates * up, axis=2)
    out = merged @ w_out
    return x + rmsnorm(out, g_post)


def setup_inputs(seed: int = 0) -> dict:
    key = jax.random.key(seed)
    ks = jax.random.split(key, 16)
    f32 = jnp.float32
    nrm = lambda k, shape, s: jax.random.normal(k, shape, f32) * s
    return {
        "x": nrm(ks[0], (BATCH, SEQ, D_MODEL), 1.0),
        "rel_bias": nrm(ks[1], (NUM_BUCKETS, DA_HEADS), 0.5),
        "g_pre": 1.0 + nrm(ks[2], (DEPTH, D_MODEL), 0.05),
        "w_in": nrm(ks[3], (DEPTH, D_MODEL, D_IN), D_MODEL ** -0.5),
        "w_gate": nrm(ks[4], (DEPTH, D_MODEL, N_BRANCH * D_MODEL), D_MODEL ** -0.5),
        "pool_w": nrm(ks[5], (DEPTH, POOL_GROUPS, POOL_GW, POOL_GW), POOL_GW ** -0.5),
        "pool_scale": 1.0 + nrm(ks[6], (DEPTH, BRANCH_W), 0.05),
        "lam": nrm(ks[7], (DEPTH, 4, DA_QK_DIM), 0.1),
        "subln_g": 1.0 + nrm(ks[8], (DEPTH, DA_V_DIM), 0.05),
        "sgu_norm_g": 1.0 + nrm(ks[9], (DEPTH, BRANCH_W), 0.05),
        "sgu_norm_b": nrm(ks[10], (DEPTH, BRANCH_W), 0.02),
        "sgu_w": nrm(ks[11], (DEPTH, SGU_GROUPS, SGU_CHUNK, SGU_CHUNK), SGU_CHUNK ** -0.5),
        "sgu_b": 1.0 + nrm(ks[12], (DEPTH, SGU_GROUPS, SGU_CHUNK), 0.02),
        "w_branch": nrm(ks[13], (DEPTH, N_BRANCH, BRANCH_W, D_MODEL), BRANCH_W ** -0.5),
        "w_out": nrm(ks[14], (DEPTH, D_MODEL, D_MODEL), D_MODEL ** -0.5),
        "g_post": 1.0 + nrm(ks[15], (DEPTH, D_MODEL), 0.05),
    }


def reference(x, rel_bias, g_pre, w_in, w_gate, pool_w, pool_scale, lam, subln_g,
              sgu_norm_g, sgu_norm_b, sgu_w, sgu_b, w_branch, w_out, g_post):
    for i in range(DEPTH):
        lambda_init = 0.8 - 0.6 * math.exp(-0.3 * i)
        x = hybrid_layer(x, rel_bias, g_pre[i], w_in[i], w_gate[i], pool_w[i], pool_scale[i],
                         lam[i], subln_g[i], sgu_norm_g[i], sgu_norm_b[i], sgu_w[i], sgu_b[i],
                         w_branch[i], w_out[i], g_post[i], lambda_init)
    return x
```

```python
import functools
import math

import numpy as np
import jax
import jax.numpy as jnp
from jax import lax
from jax.experimental import pallas as pl
from jax.experimental.pallas import tpu as pltpu

F32 = jnp.float32
BF16 = jnp.bfloat16

N_BRANCH = 3
POOL_WINDOWS = (2, 4, 8, 16)
POOL_HALO = 16
CHUNK_SHIFT = 6
DA_QK_DIM = 64
DA_V_DIM = 128
NUM_BUCKETS = 32
SGU_CHUNK = 128
SGU_GW = 128
EPS = 1e-6
NEG_INF = -1e30
T5_LOG_STARTS = (12, 16, 23, 32, 46, 64, 91)
T5_MAX_EXACT = 8
T5_FAR_BUCKET = 15

ATT_T = 256
VMEM_LIMIT = 56 * 1024 * 1024

SQRT_2_OVER_PI = float(np.float32(np.sqrt(2.0 / np.pi)))


def _cparams(n_axes):
    return pltpu.CompilerParams(dimension_semantics=("arbitrary",) * n_axes,
                                vmem_limit_bytes=VMEM_LIMIT)


def _silu(x):
    return x * (1.0 / (1.0 + jnp.exp(-x)))


def _gelu_tanh(x):
    return x * (0.5 * (1.0 + jnp.tanh(SQRT_2_OVER_PI * (x + 0.044715 * (x * x * x)))))


def _rms(x, g):
    return x * lax.rsqrt(jnp.mean(x * x, axis=-1, keepdims=True) + EPS) * g


def _prenorm_kernel(x_ref, g_ref, h_ref):
    h_ref[...] = _rms(x_ref[...], g_ref[...]).astype(BF16)


def _prenorm(x, g, tm):
    m, d = x.shape
    return pl.pallas_call(
        _prenorm_kernel,
        grid=(m // tm,),
        in_specs=[pl.BlockSpec((tm, d), lambda i: (i, 0)),
                  pl.BlockSpec((1, d), lambda i: (0, 0))],
        out_specs=pl.BlockSpec((tm, d), lambda i: (i, 0)),
        out_shape=jax.ShapeDtypeStruct((m, d), BF16),
        compiler_params=_cparams(1),
        name="prenorm",
    )(x, g)


_SILU_COLS = (1, 5, 8)
_GELU_COLS = (6, 7)


def _in_proj_kernel(h_ref, w_ref, o_ref):
    j = pl.program_id(0)
    acc = jnp.dot(h_ref[...], w_ref[...], preferred_element_type=F32)
    is_silu = (j == 1) | (j == 5) | (j == 8)
    is_gelu = (j == 6) | (j == 7)

    @pl.when(is_silu)
    def _():
        o_ref[...] = _silu(acc).astype(BF16)

    @pl.when(is_gelu)
    def _():
        o_ref[...] = _gelu_tanh(acc).astype(BF16)

    @pl.when(jnp.logical_not(is_silu | is_gelu))
    def _():
        o_ref[...] = acc.astype(BF16)


def _in_proj(h, w, tm, tn):
    m, d = h.shape
    n = w.shape[1]
    return pl.pallas_call(
        _in_proj_kernel,
        grid=(n // tn, m // tm),
        in_specs=[pl.BlockSpec((tm, d), lambda j, i: (i, 0)),
                  pl.BlockSpec((d, tn), lambda j, i: (0, j))],
        out_specs=pl.BlockSpec((tm, tn), lambda j, i: (i, j)),
        out_shape=jax.ShapeDtypeStruct((m, n), BF16),
        compiler_params=_cparams(2),
        name="in_proj",
    )(h, w)


def _pool_kernel(xh_ref, x_ref, ag_ref, w_ref, scale_ref, o_ref, *, tr, seq, gw):
    i = pl.program_id(0)
    pos0 = (i * tr) % seq
    w = w_ref[...]
    zh = jnp.dot(xh_ref[...], w, preferred_element_type=F32)
    zh = jnp.where(pos0 == 0, 0.0, zh)
    zm = jnp.dot(x_ref[...], w, preferred_element_type=F32)
    z = jnp.concatenate([zh, zm], axis=0)
    sums = []
    s = z
    for shift in (1, 2, 4, 8):
        s = s + pltpu.roll(s, shift, 0)
        sums.append(s[POOL_HALO:, :])
    t = pos0 + lax.broadcasted_iota(jnp.int32, (tr, 1), 0)
    col = lax.broadcasted_iota(jnp.int32, (1, z.shape[1]), 1)
    pooled = None
    for g in reversed(range(len(POOL_WINDOWS))):
        wl = POOL_WINDOWS[g]
        mean_g = sums[g] * (1.0 / jnp.minimum(t + 1, wl).astype(F32))
        pooled = mean_g if pooled is None else jnp.where(col < (g + 1) * gw, mean_g, pooled)
    y = (pooled - zm) * scale_ref[...] * ag_ref[...].astype(F32)
    o_ref[...] = y.astype(BF16)


def _pool(proj, w_bd, scale, seq, tr, bw):
    m = proj.shape[0]
    hb = tr // POOL_HALO
    kern = functools.partial(_pool_kernel, tr=tr, seq=seq, gw=bw // len(POOL_WINDOWS))
    return pl.pallas_call(
        kern,
        grid=(m // tr,),
        in_specs=[pl.BlockSpec((POOL_HALO, bw), lambda i: (jnp.maximum(i * hb - 1, 0), 0)),
                  pl.BlockSpec((tr, bw), lambda i: (i, 0)),
                  pl.BlockSpec((tr, bw), lambda i: (i, 1)),
                  pl.BlockSpec((bw, bw), lambda i: (0, 0)),
                  pl.BlockSpec((1, bw), lambda i: (0, 0))],
        out_specs=pl.BlockSpec((tr, bw), lambda i: (i, 0)),
        out_shape=jax.ShapeDtypeStruct((m, bw), BF16),
        compiler_params=_cparams(1),
        name="pool",
    )(proj, proj, proj, w_bd, scale)


def _bias_tile_kernel(tbl_ref, o_ref):
    h = pl.program_id(0)
    kk = lax.broadcasted_iota(jnp.int32, (ATT_T, ATT_T), 0)
    qq = lax.broadcasted_iota(jnp.int32, (ATT_T, ATT_T), 1)
    far = tbl_ref[T5_FAR_BUCKET, h]
    for tile in range(2):
        rel = kk - qq - tile * ATT_T
        n = jnp.abs(rel)
        large = jnp.full_like(n, T5_MAX_EXACT)
        for start in T5_LOG_STARTS:
            large = large + jnp.where(n >= start, 1, 0)
        bucket = jnp.where(rel > 0, NUM_BUCKETS // 2, 0) + jnp.where(n < T5_MAX_EXACT, n, large)
        bias = jnp.zeros((ATT_T, ATT_T), F32)
        for b in range(NUM_BUCKETS):
            bias = jnp.where(bucket == b, tbl_ref[b, h] - far, bias)
        if tile == 0:
            bias = jnp.where((kk >> CHUNK_SHIFT) <= (qq >> CHUNK_SHIFT), bias, NEG_INF)
        o_ref[0, tile] = bias


def _bias_tiles(rel_bias):
    heads = rel_bias.shape[1]
    return pl.pallas_call(
        _bias_tile_kernel,
        grid=(heads,),
        in_specs=[pl.BlockSpec(memory_space=pltpu.SMEM)],
        out_specs=pl.BlockSpec((1, 2, ATT_T, ATT_T), lambda h: (h, 0, 0, 0)),
        out_shape=jax.ShapeDtypeStruct((heads, 2, ATT_T, ATT_T), F32),
        compiler_params=_cparams(1),
        name="bias_tiles",
    )(rel_bias)


def _attn_kernel(lam_ref, subg_ref, q_ref, k_ref, v_ref, bg_ref, bias_ref, o_ref,
                 vt_ref, m1, l1, a1, m2, l2, a2, *, lambda_init, nkv):
    t = ATT_T
    qi = pl.program_id(2)

    @pl.when(qi == 0)
    def _():
        for c in range(nkv):
            vt_ref[c] = v_ref[c * t:(c + 1) * t, :].astype(F32).T.astype(BF16)

    lane = lax.broadcasted_iota(jnp.int32, (t, 2 * DA_QK_DIM), 1)
    qs = q_ref[...] * (DA_QK_DIM ** -0.5)
    zero = jnp.zeros_like(qs)
    halves = ((jnp.where(lane < DA_QK_DIM, qs, zero), m1, l1, a1),
              (jnp.where(lane >= DA_QK_DIM, qs, zero), m2, l2, a2))
    for _, m_ref, l_ref, a_ref in halves:
        m_ref[...] = jnp.full(m_ref.shape, NEG_INF, F32)
        l_ref[...] = jnp.zeros(l_ref.shape, F32)
        a_ref[...] = jnp.zeros(a_ref.shape, F32)

    def step(kblk, vtblk, bias):
        for qh, m_ref, l_ref, a_ref in halves:
            s = lax.dot_general(kblk, qh, (((1,), (1,)), ((), ())),
                                preferred_element_type=F32)
            if bias is not None:
                s = s + bias
            m_old = m_ref[...]
            m_new = jnp.maximum(m_old, jnp.max(s, axis=0, keepdims=True))
            alpha = jnp.exp(m_old - m_new)
            p = jnp.exp(s - m_new)
            l_ref[...] = alpha * l_ref[...] + jnp.sum(p, axis=0, keepdims=True)
            a_ref[...] = alpha * a_ref[...] + jnp.dot(vtblk, p.astype(BF16),
                                                      preferred_element_type=F32)
            m_ref[...] = m_new

    def far_body(j, carry):
        start = pl.multiple_of(j * t, t)
        step(k_ref[pl.ds(start, t), :], vt_ref[j], None)
        return carry

    lax.fori_loop(0, jnp.maximum(qi - 1, 0), far_body, 0)

    @pl.when(qi >= 1)
    def _():
        start = pl.multiple_of((qi - 1) * t, t)
        step(k_ref[pl.ds(start, t), :], vt_ref[qi - 1], bias_ref[0, 1])

    start = pl.multiple_of(qi * t, t)
    step(k_ref[pl.ds(start, t), :], vt_ref[qi], bias_ref[0, 0])

    lam = lam_ref[...]
    lam_full = (jnp.exp(jnp.sum(lam[0:1] * lam[1:2], axis=1, keepdims=True))
                - jnp.exp(jnp.sum(lam[2:3] * lam[3:4], axis=1, keepdims=True)) + lambda_init)
    o_t = a1[...] * (1.0 / l1[...]) - lam_full * (a2[...] * (1.0 / l2[...]))
    o = o_t.T
    y = _rms(o, subg_ref[...]) * (1.0 - lambda_init)
    o_ref[...] = (y * bg_ref[...].astype(F32)).astype(BF16)


def _attn(proj, bias_tiles, lam, subln_g, lambda_init, batch, seq, bw):
    m = proj.shape[0]
    t = ATT_T
    heads = bw // DA_V_DIM
    nq = seq // t
    col = bw // DA_V_DIM
    kern = functools.partial(_attn_kernel, lambda_init=lambda_init, nkv=nq)
    return pl.pallas_call(
        kern,
        grid=(batch, heads, nq),
        in_specs=[pl.BlockSpec((4, DA_QK_DIM), lambda b, h, i: (0, 0)),
                  pl.BlockSpec((1, DA_V_DIM), lambda b, h, i: (0, 0)),
                  pl.BlockSpec((t, DA_V_DIM), lambda b, h, i: (b * nq + i, 2 * col + h)),
                  pl.BlockSpec((seq, DA_V_DIM), lambda b, h, i: (b, 3 * col + h)),
                  pl.BlockSpec((seq, DA_V_DIM), lambda b, h, i: (b, 4 * col + h)),
                  pl.BlockSpec((t, DA_V_DIM), lambda b, h, i: (b * nq + i, 5 * col + h)),
                  pl.BlockSpec((1, 2, t, t), lambda b, h, i: (h, 0, 0, 0))],
        out_specs=pl.BlockSpec((t, DA_V_DIM), lambda b, h, i: (b * nq + i, h)),
        out_shape=jax.ShapeDtypeStruct((m, bw), BF16),
        scratch_shapes=[pltpu.VMEM((nq, DA_V_DIM, t), BF16),
                        pltpu.VMEM((1, t), F32), pltpu.VMEM((1, t), F32),
                        pltpu.VMEM((DA_V_DIM, t), F32),
                        pltpu.VMEM((1, t), F32), pltpu.VMEM((1, t), F32),
                        pltpu.VMEM((DA_V_DIM, t), F32)],
        compiler_params=_cparams(3),
        name="diff_attn",
    )(lam, subln_g, proj, proj, proj, proj, bias_tiles)


def _sgu_kernel(cu_ref, cv_ref, cg_ref, ng_ref, nb_ref, w_ref, bt_ref, o_ref, *, tr, groups):
    v = cv_ref[...].astype(F32)
    mu = jnp.mean(v, axis=-1, keepdims=True)
    vc = v - mu
    var = jnp.mean(vc * vc, axis=-1, keepdims=True)
    vn = (vc * lax.rsqrt(var + EPS) * ng_ref[...] + nb_ref[...]).astype(BF16)
    p = lax.broadcasted_iota(jnp.int32, (SGU_CHUNK, SGU_CHUNK), 0)
    q = lax.broadcasted_iota(jnp.int32, (SGU_CHUNK, SGU_CHUNK), 1)
    for g in range(groups):
        wm = jnp.where(q <= p, w_ref[g], 0.0).astype(BF16)
        bcol = bt_ref[:, g:g + 1]
        cs = slice(g * SGU_GW, (g + 1) * SGU_GW)
        for n in range(tr // SGU_CHUNK):
            rs = slice(n * SGU_CHUNK, (n + 1) * SGU_CHUNK)
            mixed = jnp.dot(wm, vn[rs, cs], preferred_element_type=F32) + bcol
            y = cu_ref[rs, cs].astype(F32) * mixed * cg_ref[rs, cs].astype(F32)
            o_ref[rs, cs] = y.astype(BF16)


def _sgu(proj, norm_g, norm_b, w_s, b_t, tr, bw):
    m = proj.shape[0]
    groups = bw // SGU_GW
    kern = functools.partial(_sgu_kernel, tr=tr, groups=groups)
    return pl.pallas_call(
        kern,
        grid=(m // tr,),
        in_specs=[pl.BlockSpec((tr, bw), lambda i: (i, 6)),
                  pl.BlockSpec((tr, bw), lambda i: (i, 7)),
                  pl.BlockSpec((tr, bw), lambda i: (i, 8)),
                  pl.BlockSpec((1, bw), lambda i: (0, 0)),
                  pl.BlockSpec((1, bw), lambda i: (0, 0)),
                  pl.BlockSpec((groups, SGU_CHUNK, SGU_CHUNK), lambda i: (0, 0, 0)),
                  pl.BlockSpec((SGU_CHUNK, groups), lambda i: (0, 0))],
        out_specs=pl.BlockSpec((tr, bw), lambda i: (i, 0)),
        out_shape=jax.ShapeDtypeStruct((m, bw), BF16),
        compiler_params=_cparams(1),
        name="sgu",
    )(proj, proj, proj, norm_g, norm_b, w_s, b_t)


def _merge_kernel(h_ref, ya_ref, yb_ref, yc_ref, wg0_ref, wg1_ref, wg2_ref, wb_ref, o_ref):
    h = h_ref[...]
    acc = None
    for n, (y_ref, wg_ref) in enumerate(((ya_ref, wg0_ref), (yb_ref, wg1_ref), (yc_ref, wg2_ref))):
        logits = jnp.dot(h, wg_ref[...], preferred_element_type=F32)
        gate = 1.0 / (1.0 + jnp.exp(-logits))
        up = jnp.dot(y_ref[...], wb_ref[n], preferred_element_type=F32)
        acc = gate * up if acc is None else acc + gate * up
    o_ref[...] = acc.astype(BF16)


def _merge(h, ya, yb, yc, w_gate, w_branch, tm, tn):
    m, d = h.shape
    bw = ya.shape[1]
    nb = d // tn
    y_spec = pl.BlockSpec((tm, bw), lambda j, i: (i, 0))
    return pl.pallas_call(
        _merge_kernel,
        grid=(nb, m // tm),
        in_specs=[pl.BlockSpec((tm, d), lambda j, i: (i, 0)), y_spec, y_spec, y_spec,
                  pl.BlockSpec((d, tn), lambda j, i: (0, j)),
                  pl.BlockSpec((d, tn), lambda j, i: (0, nb + j)),
                  pl.BlockSpec((d, tn), lambda j, i: (0, 2 * nb + j)),
                  pl.BlockSpec((N_BRANCH, bw, tn), lambda j, i: (0, 0, j))],
        out_specs=pl.BlockSpec((tm, tn), lambda j, i: (i, j)),
        out_shape=jax.ShapeDtypeStruct((m, d), BF16),
        compiler_params=_cparams(2),
        name="merge",
    )(h, ya, yb, yc, w_gate, w_gate, w_gate, w_branch)


def _out_kernel(m_ref, w_ref, x_ref, gpost_ref, gnext_ref, xo_ref, ho_ref):
    out = jnp.dot(m_ref[...], w_ref[...], preferred_element_type=F32)
    xn = x_ref[...] + _rms(out, gpost_ref[...])
    xo_ref[...] = xn
    ho_ref[...] = _rms(xn, gnext_ref[...]).astype(BF16)


def _out_last_kernel(m_ref, w_ref, x_ref, gpost_ref, xo_ref):
    out = jnp.dot(m_ref[...], w_ref[...], preferred_element_type=F32)
    xo_ref[...] = x_ref[...] + _rms(out, gpost_ref[...])


def _out_proj(merged, w_out, x, g_post, g_next, tm):
    m, d = x.shape
    row = pl.BlockSpec((tm, d), lambda i: (i, 0))
    vec = pl.BlockSpec((1, d), lambda i: (0, 0))
    in_specs = [row, pl.BlockSpec((d, d), lambda i: (0, 0)), row, vec]
    if g_next is None:
        return pl.pallas_call(
            _out_last_kernel, grid=(m // tm,), in_specs=in_specs, out_specs=row,
            out_shape=jax.ShapeDtypeStruct((m, d), F32),
            compiler_params=_cparams(1), name="out_proj_last",
        )(merged, w_out, x, g_post), None
    return pl.pallas_call(
        _out_kernel, grid=(m // tm,), in_specs=in_specs + [vec], out_specs=(row, row),
        out_shape=(jax.ShapeDtypeStruct((m, d), F32), jax.ShapeDtypeStruct((m, d), BF16)),
        compiler_params=_cparams(1), name="out_proj",
    )(merged, w_out, x, g_post, g_next)


def _block_diag(w):
    groups, gw, _ = w.shape
    out = jnp.zeros((groups * gw, groups * gw), w.dtype)
    for g in range(groups):
        out = lax.dynamic_update_slice(out, w[g], (g * gw, g * gw))
    return out


def kernel(x, rel_bias, g_pre, w_in, w_gate, pool_w, pool_scale, lam, subln_g, sgu_norm_g,
           sgu_norm_b, sgu_w, sgu_b, w_branch, w_out, g_post):
    batch, seq, d = x.shape
    depth = g_pre.shape[0]
    bw = w_branch.shape[2]
    m = batch * seq
    assert seq % ATT_T == 0 and bw % DA_V_DIM == 0 and w_in.shape[2] == 9 * bw
    tm = min(1024, m)
    tr = min(512, seq)

    xf = x.reshape(m, d)
    bias_tiles = _bias_tiles(rel_bias)
    h = _prenorm(xf, g_pre[0].reshape(1, d), min(512, m))
    for i in range(depth):
        lambda_init = 0.8 - 0.6 * math.exp(-0.3 * i)
        proj = _in_proj(h, w_in[i].astype(BF16), tm, bw)
        ya = _pool(proj, _block_diag(pool_w[i]).astype(BF16), pool_scale[i].reshape(1, bw),
                   seq, tr, bw)
        yb = _attn(proj, bias_tiles, lam[i], subln_g[i].reshape(1, DA_V_DIM), lambda_init,
                   batch, seq, bw)
        yc = _sgu(proj, sgu_norm_g[i].reshape(1, bw), sgu_norm_b[i].reshape(1, bw), sgu_w[i],
                  sgu_b[i].T, tr, bw)
        merged = _merge(h, ya, yb, yc, w_gate[i].astype(BF16), w_branch[i].astype(BF16),
                        tm, min(512, d))
        g_next = g_pre[i + 1].reshape(1, d) if i + 1 < depth else None
        xf, h = _out_proj(merged, w_out[i].astype(BF16), xf, g_post[i].reshape(1, d), g_next,
                          min(512, m))
    return xf.reshape(batch, seq, d)
```

```python
import functools
import math

import numpy as np
import jax
import jax.numpy as jnp
from jax import lax
from jax.experimental import pallas as pl
from jax.experimental.pallas import tpu as pltpu

F32 = jnp.float32
BF16 = jnp.bfloat16

N_BRANCH = 3
POOL_WINDOWS = (2, 4, 8, 16)
POOL_HALO = 16
CHUNK_SHIFT = 6
DA_QK_DIM = 64
DA_V_DIM = 128
NUM_BUCKETS = 32
SGU_CHUNK = 128
SGU_GW = 128
EPS = 1e-6
NEG_INF = -1e30
T5_LOG_STARTS = (12, 16, 23, 32, 46, 64, 91)
T5_MAX_EXACT = 8
T5_FAR_BUCKET = 15

ATT_T = 512
VMEM_LIMIT = 56 * 1024 * 1024

SQRT_2_OVER_PI = float(np.float32(np.sqrt(2.0 / np.pi)))


def _cparams(n_axes):
    return pltpu.CompilerParams(dimension_semantics=("arbitrary",) * n_axes,
                                vmem_limit_bytes=VMEM_LIMIT)


def _silu(x):
    return x * (1.0 / (1.0 + jnp.exp(-x)))


def _gelu_tanh(x):
    return x * (0.5 * (1.0 + jnp.tanh(SQRT_2_OVER_PI * (x + 0.044715 * (x * x * x)))))


def _rms(x, g):
    return x * lax.rsqrt(jnp.mean(x * x, axis=-1, keepdims=True) + EPS) * g


def _prenorm_kernel(x_ref, g_ref, h_ref):
    h_ref[...] = _rms(x_ref[...], g_ref[...]).astype(BF16)


def _prenorm(x, g, tm):
    m, d = x.shape
    return pl.pallas_call(
        _prenorm_kernel,
        grid=(m // tm,),
        in_specs=[pl.BlockSpec((tm, d), lambda i: (i, 0)),
                  pl.BlockSpec((1, d), lambda i: (0, 0))],
        out_specs=pl.BlockSpec((tm, d), lambda i: (i, 0)),
        out_shape=jax.ShapeDtypeStruct((m, d), BF16),
        compiler_params=_cparams(1),
        name="prenorm",
    )(x, g)


_SILU_COLS = (1, 5, 8)
_GELU_COLS = (6, 7)


def _in_proj_kernel(h_ref, w_ref, o_ref):
    j = pl.program_id(0)
    acc = jnp.dot(h_ref[...], w_ref[...], preferred_element_type=F32)
    is_silu = (j == 1) | (j == 5) | (j == 8)
    is_gelu = (j == 6) | (j == 7)

    @pl.when(is_silu)
    def _():
        o_ref[...] = _silu(acc).astype(BF16)

    @pl.when(is_gelu)
    def _():
        o_ref[...] = _gelu_tanh(acc).astype(BF16)

    @pl.when(jnp.logical_not(is_silu | is_gelu))
    def _():
        o_ref[...] = acc.astype(BF16)


def _in_proj(h, w, tm, tn):
    m, d = h.shape
    n = w.shape[1]
    return pl.pallas_call(
        _in_proj_kernel,
        grid=(n // tn, m // tm),
        in_specs=[pl.BlockSpec((tm, d), lambda j, i: (i, 0)),
                  pl.BlockSpec((d, tn), lambda j, i: (0, j))],
        out_specs=pl.BlockSpec((tm, tn), lambda j, i: (i, j)),
        out_shape=jax.ShapeDtypeStruct((m, n), BF16),
        compiler_params=_cparams(2),
        name="in_proj",
    )(h, w)


def _pool_kernel(xh_ref, x_ref, ag_ref, w_ref, scale_ref, o_ref, *, tr, seq, gw):
    i = pl.program_id(0)
    pos0 = (i * tr) % seq
    w = w_ref[...]
    zh = jnp.dot(xh_ref[...], w, preferred_element_type=F32)
    zh = jnp.where(pos0 == 0, 0.0, zh)
    zm = jnp.dot(x_ref[...], w, preferred_element_type=F32)
    z = jnp.concatenate([zh, zm], axis=0)
    sums = []
    s = z
    for shift in (1, 2, 4, 8):
        s = s + pltpu.roll(s, shift, 0)
        sums.append(s[POOL_HALO:, :])
    t = pos0 + lax.broadcasted_iota(jnp.int32, (tr, 1), 0)
    col = lax.broadcasted_iota(jnp.int32, (1, z.shape[1]), 1)
    pooled = None
    for g in reversed(range(len(POOL_WINDOWS))):
        wl = POOL_WINDOWS[g]
        mean_g = sums[g] * (1.0 / jnp.minimum(t + 1, wl).astype(F32))
        pooled = mean_g if pooled is None else jnp.where(col < (g + 1) * gw, mean_g, pooled)
    y = (pooled - zm) * scale_ref[...] * ag_ref[...].astype(F32)
    o_ref[...] = y.astype(BF16)


def _pool(proj, w_bd, scale, seq, tr, bw):
    m = proj.shape[0]
    hb = tr // POOL_HALO
    kern = functools.partial(_pool_kernel, tr=tr, seq=seq, gw=bw // len(POOL_WINDOWS))
    return pl.pallas_call(
        kern,
        grid=(m // tr,),
        in_specs=[pl.BlockSpec((POOL_HALO, bw), lambda i: (jnp.maximum(i * hb - 1, 0), 0)),
                  pl.BlockSpec((tr, bw), lambda i: (i, 0)),
                  pl.BlockSpec((tr, bw), lambda i: (i, 1)),
                  pl.BlockSpec((bw, bw), lambda i: (0, 0)),
                  pl.BlockSpec((1, bw), lambda i: (0, 0))],
        out_specs=pl.BlockSpec((tr, bw), lambda i: (i, 0)),
        out_shape=jax.ShapeDtypeStruct((m, bw), BF16),
        compiler_params=_cparams(1),
        name="pool",
    )(proj, proj, proj, w_bd, scale)


def _bias_tile_kernel(tbl_ref, o_ref):
    h = pl.program_id(0)
    kk = lax.broadcasted_iota(jnp.int32, (ATT_T, ATT_T), 0)
    qq = lax.broadcasted_iota(jnp.int32, (ATT_T, ATT_T), 1)
    far = tbl_ref[T5_FAR_BUCKET, h]
    for tile in range(2):
        rel = kk - qq - tile * ATT_T
        n = jnp.abs(rel)
        large = jnp.full_like(n, T5_MAX_EXACT)
        for start in T5_LOG_STARTS:
            large = large + jnp.where(n >= start, 1, 0)
        bucket = jnp.where(rel > 0, NUM_BUCKETS // 2, 0) + jnp.where(n < T5_MAX_EXACT, n, large)
        bias = jnp.zeros((ATT_T, ATT_T), F32)
        for b in range(NUM_BUCKETS):
            bias = jnp.where(bucket == b, tbl_ref[b, h] - far, bias)
        if tile == 0:
            bias = jnp.where((kk >> CHUNK_SHIFT) <= (qq >> CHUNK_SHIFT), bias, NEG_INF)
        o_ref[0, tile] = bias


def _bias_tiles(rel_bias):
    heads = rel_bias.shape[1]
    return pl.pallas_call(
        _bias_tile_kernel,
        grid=(heads,),
        in_specs=[pl.BlockSpec(memory_space=pltpu.SMEM)],
        out_specs=pl.BlockSpec((1, 2, ATT_T, ATT_T), lambda h: (h, 0, 0, 0)),
        out_shape=jax.ShapeDtypeStruct((heads, 2, ATT_T, ATT_T), F32),
        compiler_params=_cparams(1),
        name="bias_tiles",
    )(rel_bias)


def _attn_kernel(lam_ref, subg_ref, q_ref, k_ref, v_ref, bg_ref, bias_ref, o_ref,
                 vt_ref, m1, l1, a1, m2, l2, a2, *, lambda_init, nkv):
    t = ATT_T
    qi = pl.program_id(2)

    @pl.when(qi == 0)
    def _():
        for c in range(nkv):
            vt_ref[c] = v_ref[c * t:(c + 1) * t, :].astype(F32).T.astype(BF16)

    lane = lax.broadcasted_iota(jnp.int32, (t, 2 * DA_QK_DIM), 1)
    qs = q_ref[...] * (DA_QK_DIM ** -0.5)
    zero = jnp.zeros_like(qs)
    halves = ((jnp.where(lane < DA_QK_DIM, qs, zero), m1, l1, a1),
              (jnp.where(lane >= DA_QK_DIM, qs, zero), m2, l2, a2))
    for _, m_ref, l_ref, a_ref in halves:
        m_ref[...] = jnp.full(m_ref.shape, NEG_INF, F32)
        l_ref[...] = jnp.zeros(l_ref.shape, F32)
        a_ref[...] = jnp.zeros(a_ref.shape, F32)

    def step(kblk, vtblk, bias):
        for qh, m_ref, l_ref, a_ref in halves:
            s = lax.dot_general(kblk, qh, (((1,), (1,)), ((), ())),
                                preferred_element_type=F32)
            if bias is not None:
                s = s + bias
            m_old = m_ref[...]
            m_new = jnp.maximum(m_old, jnp.max(s, axis=0, keepdims=True))
            alpha = jnp.exp(m_old - m_new)
            p = jnp.exp(s - m_new)
            l_ref[...] = alpha * l_ref[...] + jnp.sum(p, axis=0, keepdims=True)
            a_ref[...] = alpha * a_ref[...] + jnp.dot(vtblk, p.astype(BF16),
                                                      preferred_element_type=F32)
            m_ref[...] = m_new

    def far_body(j, carry):
        start = pl.multiple_of(j * t, t)
        step(k_ref[pl.ds(start, t), :], vt_ref[j], None)
        return carry

    lax.fori_loop(0, jnp.maximum(qi - 1, 0), far_body, 0)

    @pl.when(qi >= 1)
    def _():
        start = pl.multiple_of((qi - 1) * t, t)
        step(k_ref[pl.ds(start, t), :], vt_ref[qi - 1], bias_ref[0, 1])

    start = pl.multiple_of(qi * t, t)
    step(k_ref[pl.ds(start, t), :], vt_ref[qi], bias_ref[0, 0])

    lam = lam_ref[...]
    lam_full = (jnp.exp(jnp.sum(lam[0:1] * lam[1:2], axis=1, keepdims=True))
                - jnp.exp(jnp.sum(lam[2:3] * lam[3:4], axis=1, keepdims=True)) + lambda_init)
    o_t = a1[...] * (1.0 / l1[...]) - lam_full * (a2[...] * (1.0 / l2[...]))
    o = o_t.T
    y = _rms(o, subg_ref[...]) * (1.0 - lambda_init)
    o_ref[...] = (y * bg_ref[...].astype(F32)).astype(BF16)


def _attn(proj, bias_tiles, lam, subln_g, lambda_init, batch, seq, bw):
    m = proj.shape[0]
    t = ATT_T
    heads = bw // DA_V_DIM
    nq = seq // t
    col = bw // DA_V_DIM
    kern = functools.partial(_attn_kernel, lambda_init=lambda_init, nkv=nq)
    return pl.pallas_call(
        kern,
        grid=(batch, heads, nq),
        in_specs=[pl.BlockSpec((4, DA_QK_DIM), lambda b, h, i: (0, 0)),
                  pl.BlockSpec((1, DA_V_DIM), lambda b, h, i: (0, 0)),
                  pl.BlockSpec((t, DA_V_DIM), lambda b, h, i: (b * nq + i, 2 * col + h)),
                  pl.BlockSpec((seq, DA_V_DIM), lambda b, h, i: (b, 3 * col + h)),
                  pl.BlockSpec((seq, DA_V_DIM), lambda b, h, i: (b, 4 * col + h)),
                  pl.BlockSpec((t, DA_V_DIM), lambda b, h, i: (b * nq + i, 5 * col + h)),
                  pl.BlockSpec((1, 2, t, t), lambda b, h, i: (h, 0, 0, 0))],
        out_specs=pl.BlockSpec((t, DA_V_DIM), lambda b, h, i: (b * nq + i, h)),
        out_shape=jax.ShapeDtypeStruct((m, bw), BF16),
        scratch_shapes=[pltpu.VMEM((nq, DA_V_DIM, t), BF16),
                        pltpu.VMEM((1, t), F32), pltpu.VMEM((1, t), F32),
                        pltpu.VMEM((DA_V_DIM, t), F32),
                        pltpu.VMEM((1, t), F32), pltpu.VMEM((1, t), F32),
                        pltpu.VMEM((DA_V_DIM, t), F32)],
        compiler_params=_cparams(3),
        name="diff_attn",
    )(lam, subln_g, proj, proj, proj, proj, bias_tiles)


def _sgu_kernel(cu_ref, cv_ref, cg_ref, ng_ref, nb_ref, w_ref, bt_ref, o_ref, *, tr, groups):
    v = cv_ref[...].astype(F32)
    mu = jnp.mean(v, axis=-1, keepdims=True)
    vc = v - mu
    var = jnp.mean(vc * vc, axis=-1, keepdims=True)
    vn = (vc * lax.rsqrt(var + EPS) * ng_ref[...] + nb_ref[...]).astype(BF16)
    p = lax.broadcasted_iota(jnp.int32, (SGU_CHUNK, SGU_CHUNK), 0)
    q = lax.broadcasted_iota(jnp.int32, (SGU_CHUNK, SGU_CHUNK), 1)
    for g in range(groups):
        wm = jnp.where(q <= p, w_ref[g], 0.0).astype(BF16)
        bcol = bt_ref[:, g:g + 1]
        cs = slice(g * SGU_GW, (g + 1) * SGU_GW)
        for n in range(tr // SGU_CHUNK):
            rs = slice(n * SGU_CHUNK, (n + 1) * SGU_CHUNK)
            mixed = jnp.dot(wm, vn[rs, cs], preferred_element_type=F32) + bcol
            y = cu_ref[rs, cs].astype(F32) * mixed * cg_ref[rs, cs].astype(F32)
            o_ref[rs, cs] = y.astype(BF16)


def _sgu(proj, norm_g, norm_b, w_s, b_t, tr, bw):
    m = proj.shape[0]
    groups = bw // SGU_GW
    kern = functools.partial(_sgu_kernel, tr=tr, groups=groups)
    return pl.pallas_call(
        kern,
        grid=(m // tr,),
        in_specs=[pl.BlockSpec((tr, bw), lambda i: (i, 6)),
                  pl.BlockSpec((tr, bw), lambda i: (i, 7)),
                  pl.BlockSpec((tr, bw), lambda i: (i, 8)),
                  pl.BlockSpec((1, bw), lambda i: (0, 0)),
                  pl.BlockSpec((1, bw), lambda i: (0, 0)),
                  pl.BlockSpec((groups, SGU_CHUNK, SGU_CHUNK), lambda i: (0, 0, 0)),
                  pl.BlockSpec((SGU_CHUNK, groups), lambda i: (0, 0))],
        out_specs=pl.BlockSpec((tr, bw), lambda i: (i, 0)),
        out_shape=jax.ShapeDtypeStruct((m, bw), BF16),
        compiler_params=_cparams(1),
        name="sgu",
    )(proj, proj, proj, norm_g, norm_b, w_s, b_t)


def _merge_kernel(h_ref, ya_ref, yb_ref, yc_ref, wg0_ref, wg1_ref, wg2_ref, wb_ref, o_ref):
    h = h_ref[...]
    acc = None
    for n, (y_ref, wg_ref) in enumerate(((ya_ref, wg0_ref), (yb_ref, wg1_ref), (yc_ref, wg2_ref))):
        logits = jnp.dot(h, wg_ref[...], preferred_element_type=F32)
        gate = 1.0 / (1.0 + jnp.exp(-logits))
        up = jnp.dot(y_ref[...], wb_ref[n], preferred_element_type=F32)
        acc = gate * up if acc is None else acc + gate * up
    o_ref[...] = acc.astype(BF16)


def _merge(h, ya, yb, yc, w_gate, w_branch, tm, tn):
    m, d = h.shape
    bw = ya.shape[1]
    nb = d // tn
    y_spec = pl.BlockSpec((tm, bw), lambda j, i: (i, 0))
    return pl.pallas_call(
        _merge_kernel,
        grid=(nb, m // tm),
        in_specs=[pl.BlockSpec((tm, d), lambda j, i: (i, 0)), y_spec, y_spec, y_spec,
                  pl.BlockSpec((d, tn), lambda j, i: (0, j)),
                  pl.BlockSpec((d, tn), lambda j, i: (0, nb + j)),
                  pl.BlockSpec((d, tn), lambda j, i: (0, 2 * nb + j)),
                  pl.BlockSpec((N_BRANCH, bw, tn), lambda j, i: (0, 0, j))],
        out_specs=pl.BlockSpec((tm, tn), lambda j, i: (i, j)),
        out_shape=jax.ShapeDtypeStruct((m, d), BF16),
        compiler_params=_cparams(2),
        name="merge",
    )(h, ya, yb, yc, w_gate, w_gate, w_gate, w_branch)


def _out_kernel(m_ref, w_ref, x_ref, gpost_ref, gnext_ref, xo_ref, ho_ref):
    out = jnp.dot(m_ref[...], w_ref[...], preferred_element_type=F32)
    xn = x_ref[...] + _rms(out, gpost_ref[...])
    xo_ref[...] = xn
    ho_ref[...] = _rms(xn, gnext_ref[...]).astype(BF16)


def _out_last_kernel(m_ref, w_ref, x_ref, gpost_ref, xo_ref):
    out = jnp.dot(m_ref[...], w_ref[...], preferred_element_type=F32)
    xo_ref[...] = x_ref[...] + _rms(out, gpost_ref[...])


def _out_proj(merged, w_out, x, g_post, g_next, tm):
    m, d = x.shape
    row = pl.BlockSpec((tm, d), lambda i: (i, 0))
    vec = pl.BlockSpec((1, d), lambda i: (0, 0))
    in_specs = [row, pl.BlockSpec((d, d), lambda i: (0, 0)), row, vec]
    if g_next is None:
        return pl.pallas_call(
            _out_last_kernel, grid=(m // tm,), in_specs=in_specs, out_specs=row,
            out_shape=jax.ShapeDtypeStruct((m, d), F32),
            compiler_params=_cparams(1), name="out_proj_last",
        )(merged, w_out, x, g_post), None
    return pl.pallas_call(
        _out_kernel, grid=(m // tm,), in_specs=in_specs + [vec], out_specs=(row, row),
        out_shape=(jax.ShapeDtypeStruct((m, d), F32), jax.ShapeDtypeStruct((m, d), BF16)),
        compiler_params=_cparams(1), name="out_proj",
    )(merged, w_out, x, g_post, g_next)


def _block_diag(w):
    groups, gw, _ = w.shape
    out = jnp.zeros((groups * gw, groups * gw), w.dtype)
    for g in range(groups):
        out = lax.dynamic_update_slice(out, w[g], (g * gw, g * gw))
    return out


def kernel(x, rel_bias, g_pre, w_in, w_gate, pool_w, pool_scale, lam, subln_g, sgu_norm_g,
           sgu_norm_b, sgu_w, sgu_b, w_branch, w_out, g_post):
    batch, seq, d = x.shape
    depth = g_pre.shape[0]
    bw = w_branch.shape[2]
    m = batch * seq
    assert seq % ATT_T == 0 and bw % DA_V_DIM == 0 and w_in.shape[2] == 9 * bw
    tm = min(1024, m)
    tr = min(512, seq)

    xf = x.reshape(m, d)
    bias_tiles = _bias_tiles(rel_bias)
    h = _prenorm(xf, g_pre[0].reshape(1, d), min(512, m))
    for i in range(depth):
        lambda_init = 0.8 - 0.6 * math.exp(-0.3 * i)
        proj = _in_proj(h, w_in[i].astype(BF16), tm, bw)
        ya = _pool(proj, _block_diag(pool_w[i]).astype(BF16), pool_scale[i].reshape(1, bw),
                   seq, tr, bw)
        yb = _attn(proj, bias_tiles, lam[i], subln_g[i].reshape(1, DA_V_DIM), lambda_init,
                   batch, seq, bw)
        yc = _sgu(proj, sgu_norm_g[i].reshape(1, bw), sgu_norm_b[i].reshape(1, bw), sgu_w[i],
                  sgu_b[i].T, tr, bw)
        merged = _merge(h, ya, yb, yc, w_gate[i].astype(BF16), w_branch[i].astype(BF16),
                        tm, min(512, d))
        g_next = g_pre[i + 1].reshape(1, d) if i + 1 < depth else None
        xf, h = _out_proj(merged, w_out[i].astype(BF16), xf, g_post[i].reshape(1, d), g_next,
                          min(512, m))
    return xf.reshape(batch, seq, d)
```

```python
import functools
import math

import numpy as np
import jax
import jax.numpy as jnp
from jax import lax
from jax.experimental import pallas as pl
from jax.experimental.pallas import tpu as pltpu

F32 = jnp.float32
BF16 = jnp.bfloat16

N_BRANCH = 3
POOL_WINDOWS = (2, 4, 8, 16)
POOL_HALO = 16
CHUNK_SHIFT = 6
DA_QK_DIM = 64
DA_V_DIM = 128
NUM_BUCKETS = 32
SGU_CHUNK = 128
SGU_GW = 128
EPS = 1e-6
NEG_INF = -1e30
T5_LOG_STARTS = (12, 16, 23, 32, 46, 64, 91)
T5_MAX_EXACT = 8
T5_FAR_BUCKET = 15

ATT_T = 512
assert T5_LOG_STARTS[-1] <= ATT_T
LOG2_E = math.log2(math.e)
VMEM_LIMIT = 56 * 1024 * 1024

SQRT_2_OVER_PI = float(np.float32(np.sqrt(2.0 / np.pi)))


def _cparams(n_axes):
    return pltpu.CompilerParams(dimension_semantics=("arbitrary",) * n_axes,
                                vmem_limit_bytes=VMEM_LIMIT)


def _silu(x):
    return x * (1.0 / (1.0 + jnp.exp(-x)))


def _gelu_tanh(x):
    return x * (0.5 * (1.0 + jnp.tanh(SQRT_2_OVER_PI * (x + 0.044715 * (x * x * x)))))


def _rms(x, g):
    return x * lax.rsqrt(jnp.mean(x * x, axis=-1, keepdims=True) + EPS) * g


def _prenorm_kernel(x_ref, g_ref, h_ref):
    h_ref[...] = _rms(x_ref[...], g_ref[...]).astype(BF16)


def _prenorm(x, g, tm):
    m, d = x.shape
    return pl.pallas_call(
        _prenorm_kernel,
        grid=(m // tm,),
        in_specs=[pl.BlockSpec((tm, d), lambda i: (i, 0)),
                  pl.BlockSpec((1, d), lambda i: (0, 0))],
        out_specs=pl.BlockSpec((tm, d), lambda i: (i, 0)),
        out_shape=jax.ShapeDtypeStruct((m, d), BF16),
        compiler_params=_cparams(1),
        name="prenorm",
    )(x, g)


def _in_proj_kernel(h_ref, w_ref, o_ref, wb_ref):
    j = pl.program_id(0)

    @pl.when(pl.program_id(1) == 0)
    def _():
        wb_ref[...] = w_ref[...].astype(BF16)

    acc = jnp.dot(h_ref[...], wb_ref[...], preferred_element_type=F32)
    is_silu = (j == 1) | (j == 5) | (j == 8)
    is_gelu = (j == 6) | (j == 7)

    @pl.when(is_silu)
    def _():
        o_ref[...] = _silu(acc).astype(BF16)

    @pl.when(is_gelu)
    def _():
        o_ref[...] = _gelu_tanh(acc).astype(BF16)

    @pl.when(jnp.logical_not(is_silu | is_gelu))
    def _():
        o_ref[...] = acc.astype(BF16)


def _in_proj(h, w_all, layer, tm, tn):
    m, d = h.shape
    n = w_all.shape[2]
    return pl.pallas_call(
        _in_proj_kernel,
        grid=(n // tn, m // tm),
        in_specs=[pl.BlockSpec((tm, d), lambda j, i: (i, 0)),
                  pl.BlockSpec((None, d, tn), lambda j, i: (layer, 0, j))],
        out_specs=pl.BlockSpec((tm, tn), lambda j, i: (i, j)),
        out_shape=jax.ShapeDtypeStruct((m, n), BF16),
        scratch_shapes=[pltpu.VMEM((d, tn), BF16)],
        compiler_params=_cparams(2),
        name="in_proj",
    )(h, w_all)


def _pool_kernel(xh_ref, x_ref, ag_ref, w_ref, scale_ref, o_ref, *, tr, seq, gw):
    i = pl.program_id(0)
    pos0 = (i * tr) % seq
    w = w_ref[...]
    zh = jnp.dot(xh_ref[...], w, preferred_element_type=F32)
    zh = jnp.where(pos0 == 0, 0.0, zh)
    zm = jnp.dot(x_ref[...], w, preferred_element_type=F32)
    z = jnp.concatenate([zh, zm], axis=0)
    sums = []
    s = z
    for shift in (1, 2, 4, 8):
        s = s + pltpu.roll(s, shift, 0)
        sums.append(s[POOL_HALO:, :])
    t = pos0 + lax.broadcasted_iota(jnp.int32, (tr, 1), 0)
    col = lax.broadcasted_iota(jnp.int32, (1, z.shape[1]), 1)
    pooled = None
    for g in reversed(range(len(POOL_WINDOWS))):
        wl = POOL_WINDOWS[g]
        mean_g = sums[g] * (1.0 / jnp.minimum(t + 1, wl).astype(F32))
        pooled = mean_g if pooled is None else jnp.where(col < (g + 1) * gw, mean_g, pooled)
    y = (pooled - zm) * scale_ref[...] * ag_ref[...].astype(F32)
    o_ref[...] = y.astype(BF16)


def _pool(proj, w_bd_all, layer, scale, seq, tr, bw):
    m = proj.shape[0]
    hb = tr // POOL_HALO
    kern = functools.partial(_pool_kernel, tr=tr, seq=seq, gw=bw // len(POOL_WINDOWS))
    return pl.pallas_call(
        kern,
        grid=(m // tr,),
        in_specs=[pl.BlockSpec((POOL_HALO, bw), lambda i: (jnp.maximum(i * hb - 1, 0), 0)),
                  pl.BlockSpec((tr, bw), lambda i: (i, 0)),
                  pl.BlockSpec((tr, bw), lambda i: (i, 1)),
                  pl.BlockSpec((None, bw, bw), lambda i: (layer, 0, 0)),
                  pl.BlockSpec((1, bw), lambda i: (0, 0))],
        out_specs=pl.BlockSpec((tr, bw), lambda i: (i, 0)),
        out_shape=jax.ShapeDtypeStruct((m, bw), BF16),
        compiler_params=_cparams(1),
        name="pool",
    )(proj, proj, proj, w_bd_all, scale)


def _t5_bias(tbl_ref, h, rel):
    n = jnp.abs(rel)
    large = jnp.full_like(n, T5_MAX_EXACT)
    for start in T5_LOG_STARTS:
        large = large + jnp.where(n >= start, 1, 0)
    bucket = jnp.where(rel > 0, NUM_BUCKETS // 2, 0) + jnp.where(n < T5_MAX_EXACT, n, large)
    far = tbl_ref[T5_FAR_BUCKET, h]
    bias = jnp.zeros(rel.shape, F32)
    for b in range(NUM_BUCKETS):
        bias = jnp.where(bucket == b, tbl_ref[b, h] - far, bias)
    return bias * LOG2_E


BIAS_FAR, BIAS_PREV, BIAS_DIAG = 0, 1, 2


def _bias_tile_kernel(tbl_ref, o_ref):
    h = pl.program_id(0)
    kk = lax.broadcasted_iota(jnp.int32, (ATT_T, ATT_T), 0)
    qq = lax.broadcasted_iota(jnp.int32, (ATT_T, ATT_T), 1)
    o_ref[0, BIAS_FAR] = jnp.zeros((ATT_T, ATT_T), F32)
    o_ref[0, BIAS_PREV] = _t5_bias(tbl_ref, h, kk - ATT_T - qq)
    o_ref[0, BIAS_DIAG] = jnp.where((kk >> CHUNK_SHIFT) <= (qq >> CHUNK_SHIFT),
                                    _t5_bias(tbl_ref, h, kk - qq), NEG_INF)


def _bias_tiles(rel_bias):
    heads = rel_bias.shape[1]
    return pl.pallas_call(
        _bias_tile_kernel,
        grid=(heads,),
        in_specs=[pl.BlockSpec(memory_space=pltpu.SMEM)],
        out_specs=pl.BlockSpec((1, 3, ATT_T, ATT_T), lambda h: (h, 0, 0, 0)),
        out_shape=jax.ShapeDtypeStruct((heads, 3, ATT_T, ATT_T), F32),
        compiler_params=_cparams(1),
        name="bias_tiles",
    )(rel_bias)


def _attn_kernel(lam_ref, subg_ref, q_ref, k_ref, v_ref, bg_ref, bias_ref, o_ref,
                 vt_ref, qh_ref, s_ref, mx_ref, p_ref, m_ref, l_ref, a_ref, *, lambda_init, nkv):
    t = ATT_T
    qi = pl.program_id(2)

    @pl.when(qi == 0)
    def _():
        for c in range(nkv):
            vt_ref[c] = v_ref[c * t:(c + 1) * t, :].astype(F32).T.astype(BF16)

    lane = lax.broadcasted_iota(jnp.int32, (t, 2 * DA_QK_DIM), 1)
    qs = q_ref[...] * (DA_QK_DIM ** -0.5 * LOG2_E)
    zero = jnp.zeros_like(qs)
    qh_ref[0] = jnp.where(lane < DA_QK_DIM, qs, zero)
    qh_ref[1] = jnp.where(lane >= DA_QK_DIM, qs, zero)
    m_ref[...] = jnp.full(m_ref.shape, NEG_INF, F32)
    l_ref[...] = jnp.zeros(l_ref.shape, F32)
    a_ref[...] = jnp.zeros(a_ref.shape, F32)
    p_ref[...] = jnp.zeros(p_ref.shape, BF16)

    def scores(j):
        kind = jnp.where(j == qi, BIAS_DIAG, jnp.where(j == qi - 1, BIAS_PREV, BIAS_FAR))
        kblk = k_ref[pl.ds(pl.multiple_of(j * t, t), t), :]
        for h in range(2):
            s = lax.dot_general(kblk, qh_ref[h], (((1,), (1,)), ((), ())),
                                preferred_element_type=F32) + bias_ref[0, kind]
            s_ref[h] = s
            mx_ref[h] = jnp.max(s, axis=0, keepdims=True)

    def values(j):
        vtblk = vt_ref[j]
        return tuple(jnp.dot(vtblk, p_ref[h], preferred_element_type=F32) for h in range(2))

    def softmax(pvs):
        for h in range(2):
            m_old = m_ref[h]
            m_new = jnp.maximum(m_old, mx_ref[h])
            alpha = jnp.exp2(m_old - m_new)
            p = jnp.exp2(s_ref[h] - m_new)
            l_ref[h] = alpha * l_ref[h] + jnp.sum(p, axis=0, keepdims=True)
            p_ref[h] = p.astype(BF16)
            a_ref[h] = alpha * (a_ref[h] + pvs[h])
            m_ref[h] = m_new

    def body(j, carry):
        pvs = values(jnp.maximum(j - 1, 0))
        softmax(pvs)
        scores(j + 1)
        return carry

    scores(0)
    lax.fori_loop(0, qi, body, 0)
    pvs = values(jnp.maximum(qi - 1, 0))
    softmax(pvs)
    pvs = values(qi)

    lam = lam_ref[...]
    lam_full = (jnp.exp(jnp.sum(lam[0:1] * lam[1:2], axis=1, keepdims=True))
                - jnp.exp(jnp.sum(lam[2:3] * lam[3:4], axis=1, keepdims=True)) + lambda_init)
    o1 = (a_ref[0] + pvs[0]) * (1.0 / l_ref[0])
    o2 = (a_ref[1] + pvs[1]) * (1.0 / l_ref[1])
    o = (o1 - lam_full * o2).T
    y = _rms(o, subg_ref[...]) * (1.0 - lambda_init)
    o_ref[...] = (y * bg_ref[...].astype(F32)).astype(BF16)


def _attn(proj, bias_tiles, lam, subln_g, lambda_init, batch, seq, bw):
    m = proj.shape[0]
    t = ATT_T
    heads = bw // DA_V_DIM
    nq = seq // t
    col = bw // DA_V_DIM
    kern = functools.partial(_attn_kernel, lambda_init=lambda_init, nkv=nq)
    return pl.pallas_call(
        kern,
        grid=(batch, heads, nq),
        in_specs=[pl.BlockSpec((4, DA_QK_DIM), lambda b, h, i: (0, 0)),
                  pl.BlockSpec((1, DA_V_DIM), lambda b, h, i: (0, 0)),
                  pl.BlockSpec((t, DA_V_DIM), lambda b, h, i: (b * nq + i, 2 * col + h)),
                  pl.BlockSpec((seq, DA_V_DIM), lambda b, h, i: (b, 3 * col + h)),
                  pl.BlockSpec((seq, DA_V_DIM), lambda b, h, i: (b, 4 * col + h)),
                  pl.BlockSpec((t, DA_V_DIM), lambda b, h, i: (b * nq + i, 5 * col + h)),
                  pl.BlockSpec((1, 3, t, t), lambda b, h, i: (h, 0, 0, 0))],
        out_specs=pl.BlockSpec((t, DA_V_DIM), lambda b, h, i: (b * nq + i, h)),
        out_shape=jax.ShapeDtypeStruct((m, bw), BF16),
        scratch_shapes=[pltpu.VMEM((nq, DA_V_DIM, t), BF16),
                        pltpu.VMEM((2, t, 2 * DA_QK_DIM), BF16),
                        pltpu.VMEM((2, t, t), F32),
                        pltpu.VMEM((2, 1, t), F32),
                        pltpu.VMEM((2, t, t), BF16),
                        pltpu.VMEM((2, 1, t), F32),
                        pltpu.VMEM((2, 1, t), F32),
                        pltpu.VMEM((2, DA_V_DIM, t), F32)],
        compiler_params=_cparams(3),
        name="diff_attn",
    )(lam, subln_g, proj, proj, proj, proj, bias_tiles)


def _sgu_kernel(cu_ref, cv_ref, cg_ref, ng_ref, nb_ref, w_ref, bt_ref, o_ref, *, tr, groups):
    v = cv_ref[...].astype(F32)
    mu = jnp.mean(v, axis=-1, keepdims=True)
    vc = v - mu
    var = jnp.mean(vc * vc, axis=-1, keepdims=True)
    vn = (vc * lax.rsqrt(var + EPS) * ng_ref[...] + nb_ref[...]).astype(BF16)
    p = lax.broadcasted_iota(jnp.int32, (SGU_CHUNK, SGU_CHUNK), 0)
    q = lax.broadcasted_iota(jnp.int32, (SGU_CHUNK, SGU_CHUNK), 1)
    for g in range(groups):
        wm = jnp.where(q <= p, w_ref[g], 0.0).astype(BF16)
        bcol = bt_ref[:, g:g + 1]
        cs = slice(g * SGU_GW, (g + 1) * SGU_GW)
        for n in range(tr // SGU_CHUNK):
            rs = slice(n * SGU_CHUNK, (n + 1) * SGU_CHUNK)
            mixed = jnp.dot(wm, vn[rs, cs], preferred_element_type=F32) + bcol
            y = cu_ref[rs, cs].astype(F32) * mixed * cg_ref[rs, cs].astype(F32)
            o_ref[rs, cs] = y.astype(BF16)


def _sgu(proj, norm_g, norm_b, w_s, b_t, tr, bw):
    m = proj.shape[0]
    groups = bw // SGU_GW
    kern = functools.partial(_sgu_kernel, tr=tr, groups=groups)
    return pl.pallas_call(
        kern,
        grid=(m // tr,),
        in_specs=[pl.BlockSpec((tr, bw), lambda i: (i, 6)),
                  pl.BlockSpec((tr, bw), lambda i: (i, 7)),
                  pl.BlockSpec((tr, bw), lambda i: (i, 8)),
                  pl.BlockSpec((1, bw), lambda i: (0, 0)),
                  pl.BlockSpec((1, bw), lambda i: (0, 0)),
                  pl.BlockSpec((groups, SGU_CHUNK, SGU_CHUNK), lambda i: (0, 0, 0)),
                  pl.BlockSpec((SGU_CHUNK, groups), lambda i: (0, 0))],
        out_specs=pl.BlockSpec((tr, bw), lambda i: (i, 0)),
        out_shape=jax.ShapeDtypeStruct((m, bw), BF16),
        compiler_params=_cparams(1),
        name="sgu",
    )(proj, proj, proj, norm_g, norm_b, w_s, b_t)


def _merge_kernel(h_ref, ya_ref, yb_ref, yc_ref, wg0_ref, wg1_ref, wg2_ref, wb_ref, o_ref,
                  wgb_ref, wbb_ref):
    @pl.when(pl.program_id(1) == 0)
    def _():
        for n, wg_ref in enumerate((wg0_ref, wg1_ref, wg2_ref)):
            wgb_ref[n] = wg_ref[...].astype(BF16)
        wbb_ref[...] = wb_ref[...].astype(BF16)

    h = h_ref[...]
    acc = None
    for n, y_ref in enumerate((ya_ref, yb_ref, yc_ref)):
        logits = jnp.dot(h, wgb_ref[n], preferred_element_type=F32)
        gate = 1.0 / (1.0 + jnp.exp(-logits))
        up = jnp.dot(y_ref[...], wbb_ref[n], preferred_element_type=F32)
        acc = gate * up if acc is None else acc + gate * up
    o_ref[...] = acc.astype(BF16)


def _merge(h, ya, yb, yc, w_gate_all, w_branch_all, layer, tm, tn):
    m, d = h.shape
    bw = ya.shape[1]
    nb = d // tn
    y_spec = pl.BlockSpec((tm, bw), lambda j, i: (i, 0))
    return pl.pallas_call(
        _merge_kernel,
        grid=(nb, m // tm),
        in_specs=[pl.BlockSpec((tm, d), lambda j, i: (i, 0)), y_spec, y_spec, y_spec,
                  pl.BlockSpec((None, d, tn), lambda j, i: (layer, 0, j)),
                  pl.BlockSpec((None, d, tn), lambda j, i: (layer, 0, nb + j)),
                  pl.BlockSpec((None, d, tn), lambda j, i: (layer, 0, 2 * nb + j)),
                  pl.BlockSpec((None, N_BRANCH, bw, tn), lambda j, i: (layer, 0, 0, j))],
        out_specs=pl.BlockSpec((tm, tn), lambda j, i: (i, j)),
        out_shape=jax.ShapeDtypeStruct((m, d), BF16),
        scratch_shapes=[pltpu.VMEM((N_BRANCH, d, tn), BF16),
                        pltpu.VMEM((N_BRANCH, bw, tn), BF16)],
        compiler_params=_cparams(2),
        name="merge",
    )(h, ya, yb, yc, w_gate_all, w_gate_all, w_gate_all, w_branch_all)


def _out_kernel(m_ref, w_ref, x_ref, gpost_ref, gnext_ref, xo_ref, ho_ref):
    out = jnp.dot(m_ref[...], w_ref[...], preferred_element_type=F32)
    xn = x_ref[...] + _rms(out, gpost_ref[...])
    xo_ref[...] = xn
    ho_ref[...] = _rms(xn, gnext_ref[...]).astype(BF16)


def _out_last_kernel(m_ref, w_ref, x_ref, gpost_ref, xo_ref):
    out = jnp.dot(m_ref[...], w_ref[...], preferred_element_type=F32)
    xo_ref[...] = x_ref[...] + _rms(out, gpost_ref[...])


def _out_proj(merged, w_out_all, layer, x, g_post, g_next, tm):
    m, d = x.shape
    row = pl.BlockSpec((tm, d), lambda i: (i, 0))
    vec = pl.BlockSpec((1, d), lambda i: (0, 0))
    in_specs = [row, pl.BlockSpec((None, d, d), lambda i: (layer, 0, 0)), row, vec]
    if g_next is None:
        return pl.pallas_call(
            _out_last_kernel, grid=(m // tm,), in_specs=in_specs, out_specs=row,
            out_shape=jax.ShapeDtypeStruct((m, d), F32),
            compiler_params=_cparams(1), name="out_proj_last",
        )(merged, w_out_all, x, g_post), None
    return pl.pallas_call(
        _out_kernel, grid=(m // tm,), in_specs=in_specs + [vec], out_specs=(row, row),
        out_shape=(jax.ShapeDtypeStruct((m, d), F32), jax.ShapeDtypeStruct((m, d), BF16)),
        compiler_params=_cparams(1), name="out_proj",
    )(merged, w_out_all, x, g_post, g_next)


def _block_diag(w):
    depth, groups, gw, _ = w.shape
    out = jnp.zeros((depth, groups * gw, groups * gw), w.dtype)
    for g in range(groups):
        out = lax.dynamic_update_slice(out, w[:, g], (0, g * gw, g * gw))
    return out


def kernel(x, rel_bias, g_pre, w_in, w_gate, pool_w, pool_scale, lam, subln_g, sgu_norm_g,
           sgu_norm_b, sgu_w, sgu_b, w_branch, w_out, g_post):
    batch, seq, d = x.shape
    depth = g_pre.shape[0]
    bw = w_branch.shape[2]
    m = batch * seq
    assert seq % ATT_T == 0 and bw % DA_V_DIM == 0 and w_in.shape[2] == 9 * bw
    tm = min(1024, m)
    tr = min(512, seq)

    xf = x.reshape(m, d)
    bias_tiles = _bias_tiles(rel_bias)
    pool_w_bd = _block_diag(pool_w).astype(BF16)
    w_out_bf = w_out.astype(BF16)
    h = _prenorm(xf, g_pre[0].reshape(1, d), min(512, m))
    for i in range(depth):
        lambda_init = 0.8 - 0.6 * math.exp(-0.3 * i)
        proj = _in_proj(h, w_in, i, tm, bw)
        ya = _pool(proj, pool_w_bd, i, pool_scale[i].reshape(1, bw), seq, tr, bw)
        yb = _attn(proj, bias_tiles, lam[i], subln_g[i].reshape(1, DA_V_DIM), lambda_init,
                   batch, seq, bw)
        yc = _sgu(proj, sgu_norm_g[i].reshape(1, bw), sgu_norm_b[i].reshape(1, bw), sgu_w[i],
                  sgu_b[i].T, tr, bw)
        merged = _merge(h, ya, yb, yc, w_gate, w_branch, i, tm, min(256, d))
        g_next = g_pre[i + 1].reshape(1, d) if i + 1 < depth else None
        xf, h = _out_proj(merged, w_out_bf, i, xf, g_post[i].reshape(1, d), g_next, min(512, m))
    return xf.reshape(batch, seq, d)
```

```python
import functools
import math

import numpy as np
import jax
import jax.numpy as jnp
from jax import lax
from jax.experimental import pallas as pl
from jax.experimental.pallas import tpu as pltpu

F32 = jnp.float32
BF16 = jnp.bfloat16

N_BRANCH = 3
POOL_WINDOWS = (2, 4, 8, 16)
POOL_HALO = 16
CHUNK_SHIFT = 6
DA_QK_DIM = 64
DA_V_DIM = 128
NUM_BUCKETS = 32
SGU_CHUNK = 128
SGU_GW = 128
EPS = 1e-6
NEG_INF = -1e30
T5_LOG_STARTS = (12, 16, 23, 32, 46, 64, 91)
T5_MAX_EXACT = 8
T5_FAR_BUCKET = 15

ATT_T = 512
ATT_HEADS_PER_STEP = 3
assert T5_LOG_STARTS[-1] <= ATT_T
LOG2_E = math.log2(math.e)
VMEM_LIMIT = 56 * 1024 * 1024

SQRT_2_OVER_PI = float(np.float32(np.sqrt(2.0 / np.pi)))


def _cparams(n_axes):
    return pltpu.CompilerParams(dimension_semantics=("arbitrary",) * n_axes,
                                vmem_limit_bytes=VMEM_LIMIT)


def _rms(x, g):
    return x * lax.rsqrt(jnp.mean(x * x, axis=-1, keepdims=True) + EPS) * g


def _prenorm_kernel(x_ref, g_ref, h_ref):
    h_ref[...] = _rms(x_ref[...], g_ref[...]).astype(BF16)


def _prenorm(x, g, tm):
    m, d = x.shape
    return pl.pallas_call(
        _prenorm_kernel,
        grid=(m // tm,),
        in_specs=[pl.BlockSpec((tm, d), lambda i: (i, 0)),
                  pl.BlockSpec((1, d), lambda i: (0, 0))],
        out_specs=pl.BlockSpec((tm, d), lambda i: (i, 0)),
        out_shape=jax.ShapeDtypeStruct((m, d), BF16),
        compiler_params=_cparams(1),
        name="prenorm",
    )(x, g)


def _in_proj_kernel(h_ref, w_ref, o_ref, wb_ref):
    j = pl.program_id(0)

    @pl.when(pl.program_id(1) == 0)
    def _():
        wb_ref[...] = w_ref[...].astype(BF16)

    is_silu = (j == 1) | (j == 5) | (j == 8)
    is_gelu = (j == 6) | (j == 7)
    a = jnp.where(is_silu, 0.5, jnp.where(is_gelu, SQRT_2_OVER_PI, 0.0))
    b = jnp.where(is_gelu, SQRT_2_OVER_PI * 0.044715, 0.0)
    beta = jnp.where(is_silu | is_gelu, 0.5, 0.0)
    x = jnp.dot(h_ref[...], wb_ref[...], preferred_element_type=F32)
    o_ref[...] = (x * ((1.0 - beta) + beta * jnp.tanh(x * (a + b * (x * x))))).astype(BF16)


def _in_proj(h, w_all, layer, tm, tn):
    m, d = h.shape
    n = w_all.shape[2]
    return pl.pallas_call(
        _in_proj_kernel,
        grid=(n // tn, m // tm),
        in_specs=[pl.BlockSpec((tm, d), lambda j, i: (i, 0)),
                  pl.BlockSpec((None, d, tn), lambda j, i: (layer, 0, j))],
        out_specs=pl.BlockSpec((tm, tn), lambda j, i: (i, j)),
        out_shape=jax.ShapeDtypeStruct((m, n), BF16),
        scratch_shapes=[pltpu.VMEM((d, tn), BF16)],
        compiler_params=_cparams(2),
        name="in_proj",
    )(h, w_all)


def _pool_kernel(xh_ref, x_ref, ag_ref, w_ref, scale_ref, o_ref, *, tr, seq, gw):
    i = pl.program_id(0)
    pos0 = (i * tr) % seq
    w = w_ref[...]
    zh = jnp.dot(xh_ref[...], w, preferred_element_type=F32)
    zh = jnp.where(pos0 == 0, 0.0, zh)
    zm = jnp.dot(x_ref[...], w, preferred_element_type=F32)
    z = jnp.concatenate([zh, zm], axis=0)
    sums = []
    s = z
    for shift in (1, 2, 4, 8):
        s = s + pltpu.roll(s, shift, 0)
        sums.append(s[POOL_HALO:, :])
    t = pos0 + lax.broadcasted_iota(jnp.int32, (tr, 1), 0)
    col = lax.broadcasted_iota(jnp.int32, (1, z.shape[1]), 1)
    pooled = None
    for g in reversed(range(len(POOL_WINDOWS))):
        wl = POOL_WINDOWS[g]
        mean_g = sums[g] * (1.0 / jnp.minimum(t + 1, wl).astype(F32))
        pooled = mean_g if pooled is None else jnp.where(col < (g + 1) * gw, mean_g, pooled)
    y = (pooled - zm) * scale_ref[...] * ag_ref[...].astype(F32)
    o_ref[...] = y.astype(BF16)


def _pool(proj, w_bd_all, layer, scale, seq, tr, bw):
    m = proj.shape[0]
    hb = tr // POOL_HALO
    kern = functools.partial(_pool_kernel, tr=tr, seq=seq, gw=bw // len(POOL_WINDOWS))
    return pl.pallas_call(
        kern,
        grid=(m // tr,),
        in_specs=[pl.BlockSpec((POOL_HALO, bw), lambda i: (jnp.maximum(i * hb - 1, 0), 0)),
                  pl.BlockSpec((tr, bw), lambda i: (i, 0)),
                  pl.BlockSpec((tr, bw), lambda i: (i, 1)),
                  pl.BlockSpec((None, bw, bw), lambda i: (layer, 0, 0)),
                  pl.BlockSpec((1, bw), lambda i: (0, 0))],
        out_specs=pl.BlockSpec((tr, bw), lambda i: (i, 0)),
        out_shape=jax.ShapeDtypeStruct((m, bw), BF16),
        compiler_params=_cparams(1),
        name="pool",
    )(proj, proj, proj, w_bd_all, scale)


def _t5_bias(tbl_ref, h, rel):
    n = jnp.abs(rel)
    large = jnp.full_like(n, T5_MAX_EXACT)
    for start in T5_LOG_STARTS:
        large = large + jnp.where(n >= start, 1, 0)
    bucket = jnp.where(rel > 0, NUM_BUCKETS // 2, 0) + jnp.where(n < T5_MAX_EXACT, n, large)
    far = tbl_ref[T5_FAR_BUCKET, h]
    bias = jnp.zeros(rel.shape, F32)
    for b in range(NUM_BUCKETS):
        bias = jnp.where(bucket == b, tbl_ref[b, h] - far, bias)
    return bias * LOG2_E


BIAS_FAR, BIAS_PREV, BIAS_DIAG = 0, 1, 2


def _bias_tile_kernel(tbl_ref, o_ref):
    h = pl.program_id(0)
    kk = lax.broadcasted_iota(jnp.int32, (ATT_T, ATT_T), 0)
    qq = lax.broadcasted_iota(jnp.int32, (ATT_T, ATT_T), 1)
    o_ref[0, BIAS_FAR] = jnp.zeros((ATT_T, ATT_T), F32)
    o_ref[0, BIAS_PREV] = _t5_bias(tbl_ref, h, kk - ATT_T - qq)
    o_ref[0, BIAS_DIAG] = jnp.where((kk >> CHUNK_SHIFT) <= (qq >> CHUNK_SHIFT),
                                    _t5_bias(tbl_ref, h, kk - qq), NEG_INF)


def _bias_tiles(rel_bias):
    heads = rel_bias.shape[1]
    return pl.pallas_call(
        _bias_tile_kernel,
        grid=(heads,),
        in_specs=[pl.BlockSpec(memory_space=pltpu.SMEM)],
        out_specs=pl.BlockSpec((1, 3, ATT_T, ATT_T), lambda h: (h, 0, 0, 0)),
        out_shape=jax.ShapeDtypeStruct((heads, 3, ATT_T, ATT_T), F32),
        compiler_params=_cparams(1),
        name="bias_tiles",
    )(rel_bias)


def _attn_kernel(lam_ref, subg_ref, q_ref, k_ref, v_ref, bg_ref, bias_ref, o_ref,
                 vt_ref, qh_ref, s_ref, mx_ref, p_ref, m_ref, l_ref, a_ref, *, lambda_init, nkv,
                 nh):
    t = ATT_T
    dv = DA_V_DIM
    qi = pl.program_id(2)
    chains = [(hd, half) for hd in range(nh) for half in range(2)]

    @pl.when(qi == 0)
    def _():
        for hd in range(nh):
            for c in range(nkv):
                vt_ref[hd, c] = (v_ref[c * t:(c + 1) * t, hd * dv:(hd + 1) * dv]
                                 .astype(F32).T.astype(BF16))

    lane = lax.broadcasted_iota(jnp.int32, (t, 2 * DA_QK_DIM), 1)
    for hd in range(nh):
        qs = q_ref[:, hd * dv:(hd + 1) * dv] * (DA_QK_DIM ** -0.5 * LOG2_E)
        zero = jnp.zeros_like(qs)
        qh_ref[2 * hd] = jnp.where(lane < DA_QK_DIM, qs, zero)
        qh_ref[2 * hd + 1] = jnp.where(lane >= DA_QK_DIM, qs, zero)
    m_ref[...] = jnp.full(m_ref.shape, NEG_INF, F32)
    l_ref[...] = jnp.zeros(l_ref.shape, F32)
    a_ref[...] = jnp.zeros(a_ref.shape, F32)
    p_ref[...] = jnp.zeros(p_ref.shape, BF16)

    def scores(j):
        kind = jnp.where(j == qi, BIAS_DIAG, jnp.where(j == qi - 1, BIAS_PREV, BIAS_FAR))
        rows = pl.ds(pl.multiple_of(j * t, t), t)
        for c, (hd, _) in enumerate(chains):
            s = lax.dot_general(k_ref[rows, hd * dv:(hd + 1) * dv], qh_ref[c],
                                (((1,), (1,)), ((), ())),
                                preferred_element_type=F32) + bias_ref[hd, kind]
            s_ref[c] = s
            mx_ref[c] = jnp.max(s, axis=0, keepdims=True)

    def values(j):
        return tuple(jnp.dot(vt_ref[hd, j], p_ref[c], preferred_element_type=F32)
                     for c, (hd, _) in enumerate(chains))

    def softmax(pvs):
        for c in range(len(chains)):
            m_old = m_ref[c]
            m_new = jnp.maximum(m_old, mx_ref[c])
            alpha = jnp.exp2(m_old - m_new)
            p = jnp.exp2(s_ref[c] - m_new)
            l_ref[c] = alpha * l_ref[c] + jnp.sum(p, axis=0, keepdims=True)
            p_ref[c] = p.astype(BF16)
            a_ref[c] = alpha * (a_ref[c] + pvs[c])
            m_ref[c] = m_new

    def body(j, carry):
        pvs = values(jnp.maximum(j - 1, 0))
        softmax(pvs)
        scores(j + 1)
        return carry

    scores(0)
    lax.fori_loop(0, qi, body, 0)
    pvs = values(jnp.maximum(qi - 1, 0))
    softmax(pvs)
    pvs = values(qi)

    lam = lam_ref[...]
    lam_full = (jnp.exp(jnp.sum(lam[0:1] * lam[1:2], axis=1, keepdims=True))
                - jnp.exp(jnp.sum(lam[2:3] * lam[3:4], axis=1, keepdims=True)) + lambda_init)
    for hd in range(nh):
        c1, c2 = 2 * hd, 2 * hd + 1
        o1 = (a_ref[c1] + pvs[c1]) * (1.0 / l_ref[c1])
        o2 = (a_ref[c2] + pvs[c2]) * (1.0 / l_ref[c2])
        o = (o1 - lam_full * o2).T
        y = _rms(o, subg_ref[...]) * (1.0 - lambda_init)
        cols = slice(hd * dv, (hd + 1) * dv)
        o_ref[:, cols] = (y * bg_ref[:, cols].astype(F32)).astype(BF16)


def _attn(proj, bias_tiles, lam, subln_g, lambda_init, batch, seq, bw):
    m = proj.shape[0]
    t = ATT_T
    heads = bw // DA_V_DIM
    nh = ATT_HEADS_PER_STEP
    assert heads % nh == 0
    nq = seq // t
    col = bw // (nh * DA_V_DIM)
    gw = nh * DA_V_DIM
    nc = 2 * nh
    kern = functools.partial(_attn_kernel, lambda_init=lambda_init, nkv=nq, nh=nh)
    return pl.pallas_call(
        kern,
        grid=(batch, heads // nh, nq),
        in_specs=[pl.BlockSpec((4, DA_QK_DIM), lambda b, h, i: (0, 0)),
                  pl.BlockSpec((1, DA_V_DIM), lambda b, h, i: (0, 0)),
                  pl.BlockSpec((t, gw), lambda b, h, i: (b * nq + i, 2 * col + h)),
                  pl.BlockSpec((seq, gw), lambda b, h, i: (b, 3 * col + h)),
                  pl.BlockSpec((seq, gw), lambda b, h, i: (b, 4 * col + h)),
                  pl.BlockSpec((t, gw), lambda b, h, i: (b * nq + i, 5 * col + h)),
                  pl.BlockSpec((nh, 3, t, t), lambda b, h, i: (h, 0, 0, 0))],
        out_specs=pl.BlockSpec((t, gw), lambda b, h, i: (b * nq + i, h)),
        out_shape=jax.ShapeDtypeStruct((m, bw), BF16),
        scratch_shapes=[pltpu.VMEM((nh, nq, DA_V_DIM, t), BF16),
                        pltpu.VMEM((nc, t, 2 * DA_QK_DIM), BF16),
                        pltpu.VMEM((nc, t, t), F32),
                        pltpu.VMEM((nc, 1, t), F32),
                        pltpu.VMEM((nc, t, t), BF16),
                        pltpu.VMEM((nc, 1, t), F32),
                        pltpu.VMEM((nc, 1, t), F32),
                        pltpu.VMEM((nc, DA_V_DIM, t), F32)],
        compiler_params=_cparams(3),
        name="diff_attn",
    )(lam, subln_g, proj, proj, proj, proj, bias_tiles)


def _sgu_kernel(cu_ref, cv_ref, cg_ref, ng_ref, nb_ref, w_ref, bt_ref, o_ref, *, tr, groups):
    v = cv_ref[...].astype(F32)
    mu = jnp.mean(v, axis=-1, keepdims=True)
    vc = v - mu
    var = jnp.mean(vc * vc, axis=-1, keepdims=True)
    vn = (vc * lax.rsqrt(var + EPS) * ng_ref[...] + nb_ref[...]).astype(BF16)
    p = lax.broadcasted_iota(jnp.int32, (SGU_CHUNK, SGU_CHUNK), 0)
    q = lax.broadcasted_iota(jnp.int32, (SGU_CHUNK, SGU_CHUNK), 1)
    for g in range(groups):
        wm = jnp.where(q <= p, w_ref[g], 0.0).astype(BF16)
        bcol = bt_ref[:, g:g + 1]
        cs = slice(g * SGU_GW, (g + 1) * SGU_GW)
        for n in range(tr // SGU_CHUNK):
            rs = slice(n * SGU_CHUNK, (n + 1) * SGU_CHUNK)
            mixed = jnp.dot(wm, vn[rs, cs], preferred_element_type=F32) + bcol
            y = cu_ref[rs, cs].astype(F32) * mixed * cg_ref[rs, cs].astype(F32)
            o_ref[rs, cs] = y.astype(BF16)


def _sgu(proj, norm_g, norm_b, w_s, b_t, tr, bw):
    m = proj.shape[0]
    groups = bw // SGU_GW
    kern = functools.partial(_sgu_kernel, tr=tr, groups=groups)
    return pl.pallas_call(
        kern,
        grid=(m // tr,),
        in_specs=[pl.BlockSpec((tr, bw), lambda i: (i, 6)),
                  pl.BlockSpec((tr, bw), lambda i: (i, 7)),
                  pl.BlockSpec((tr, bw), lambda i: (i, 8)),
                  pl.BlockSpec((1, bw), lambda i: (0, 0)),
                  pl.BlockSpec((1, bw), lambda i: (0, 0)),
                  pl.BlockSpec((groups, SGU_CHUNK, SGU_CHUNK), lambda i: (0, 0, 0)),
                  pl.BlockSpec((SGU_CHUNK, groups), lambda i: (0, 0))],
        out_specs=pl.BlockSpec((tr, bw), lambda i: (i, 0)),
        out_shape=jax.ShapeDtypeStruct((m, bw), BF16),
        compiler_params=_cparams(1),
        name="sgu",
    )(proj, proj, proj, norm_g, norm_b, w_s, b_t)


def _merge_kernel(h_ref, ya_ref, yb_ref, yc_ref, wg0_ref, wg1_ref, wg2_ref, wb_ref, o_ref,
                  wgb_ref, wbb_ref):
    @pl.when(pl.program_id(1) == 0)
    def _():
        for n, wg_ref in enumerate((wg0_ref, wg1_ref, wg2_ref)):
            wgb_ref[n] = wg_ref[...].astype(BF16)
        wbb_ref[...] = wb_ref[...].astype(BF16)

    h = h_ref[...]
    acc = None
    for n, y_ref in enumerate((ya_ref, yb_ref, yc_ref)):
        logits = jnp.dot(h, wgb_ref[n], preferred_element_type=F32)
        gate = 1.0 / (1.0 + jnp.exp(-logits))
        up = jnp.dot(y_ref[...], wbb_ref[n], preferred_element_type=F32)
        acc = gate * up if acc is None else acc + gate * up
    o_ref[...] = acc.astype(BF16)


def _merge(h, ya, yb, yc, w_gate_all, w_branch_all, layer, tm, tn):
    m, d = h.shape
    bw = ya.shape[1]
    nb = d // tn
    y_spec = pl.BlockSpec((tm, bw), lambda j, i: (i, 0))
    return pl.pallas_call(
        _merge_kernel,
        grid=(nb, m // tm),
        in_specs=[pl.BlockSpec((tm, d), lambda j, i: (i, 0)), y_spec, y_spec, y_spec,
                  pl.BlockSpec((None, d, tn), lambda j, i: (layer, 0, j)),
                  pl.BlockSpec((None, d, tn), lambda j, i: (layer, 0, nb + j)),
                  pl.BlockSpec((None, d, tn), lambda j, i: (layer, 0, 2 * nb + j)),
                  pl.BlockSpec((None, N_BRANCH, bw, tn), lambda j, i: (layer, 0, 0, j))],
        out_specs=pl.BlockSpec((tm, tn), lambda j, i: (i, j)),
        out_shape=jax.ShapeDtypeStruct((m, d), BF16),
        scratch_shapes=[pltpu.VMEM((N_BRANCH, d, tn), BF16),
                        pltpu.VMEM((N_BRANCH, bw, tn), BF16)],
        compiler_params=_cparams(2),
        name="merge",
    )(h, ya, yb, yc, w_gate_all, w_gate_all, w_gate_all, w_branch_all)


def _out_kernel(m_ref, w_ref, x_ref, gpost_ref, gnext_ref, xo_ref, ho_ref):
    out = jnp.dot(m_ref[...], w_ref[...], preferred_element_type=F32)
    xn = x_ref[...] + _rms(out, gpost_ref[...])
    xo_ref[...] = xn
    ho_ref[...] = _rms(xn, gnext_ref[...]).astype(BF16)


def _out_last_kernel(m_ref, w_ref, x_ref, gpost_ref, xo_ref):
    out = jnp.dot(m_ref[...], w_ref[...], preferred_element_type=F32)
    xo_ref[...] = x_ref[...] + _rms(out, gpost_ref[...])


def _out_proj(merged, w_out_all, layer, x, g_post, g_next, tm):
    m, d = x.shape
    row = pl.BlockSpec((tm, d), lambda i: (i, 0))
    vec = pl.BlockSpec((1, d), lambda i: (0, 0))
    in_specs = [row, pl.BlockSpec((None, d, d), lambda i: (layer, 0, 0)), row, vec]
    if g_next is None:
        return pl.pallas_call(
            _out_last_kernel, grid=(m // tm,), in_specs=in_specs, out_specs=row,
            out_shape=jax.ShapeDtypeStruct((m, d), F32),
            compiler_params=_cparams(1), name="out_proj_last",
        )(merged, w_out_all, x, g_post), None
    return pl.pallas_call(
        _out_kernel, grid=(m // tm,), in_specs=in_specs + [vec], out_specs=(row, row),
        out_shape=(jax.ShapeDtypeStruct((m, d), F32), jax.ShapeDtypeStruct((m, d), BF16)),
        compiler_params=_cparams(1), name="out_proj",
    )(merged, w_out_all, x, g_post, g_next)


def _block_diag(w):
    depth, groups, gw, _ = w.shape
    eye = jnp.eye(groups, dtype=w.dtype)
    return (w[:, :, :, None, :] * eye[None, :, None, :, None]).reshape(
        depth, groups * gw, groups * gw)


def kernel(x, rel_bias, g_pre, w_in, w_gate, pool_w, pool_scale, lam, subln_g, sgu_norm_g,
           sgu_norm_b, sgu_w, sgu_b, w_branch, w_out, g_post):
    batch, seq, d = x.shape
    depth = g_pre.shape[0]
    bw = w_branch.shape[2]
    m = batch * seq
    assert seq % ATT_T == 0 and bw % DA_V_DIM == 0 and w_in.shape[2] == 9 * bw
    tm = min(1024, m)
    tr = min(512, seq)

    xf = x.reshape(m, d)
    bias_tiles = _bias_tiles(rel_bias)
    pool_w_bd = _block_diag(pool_w).astype(BF16)
    w_out_bf = w_out.astype(BF16)
    h = _prenorm(xf, g_pre[0].reshape(1, d), min(512, m))
    for i in range(depth):
        lambda_init = 0.8 - 0.6 * math.exp(-0.3 * i)
        proj = _in_proj(h, w_in, i, tm, bw)
        ya = _pool(proj, pool_w_bd, i, pool_scale[i].reshape(1, bw), seq, tr, bw)
        yb = _attn(proj, bias_tiles, lam[i], subln_g[i].reshape(1, DA_V_DIM), lambda_init,
                   batch, seq, bw)
        yc = _sgu(proj, sgu_norm_g[i].reshape(1, bw), sgu_norm_b[i].reshape(1, bw), sgu_w[i],
                  sgu_b[i].T, tr, bw)
        merged = _merge(h, ya, yb, yc, w_gate, w_branch, i, tm, min(256, d))
        g_next = g_pre[i + 1].reshape(1, d) if i + 1 < depth else None
        xf, h = _out_proj(merged, w_out_bf, i, xf, g_post[i].reshape(1, d), g_next, min(512, m))
    return xf.reshape(batch, seq, d)
```
